```python
import math
import jax, jax.numpy as jnp
from jax import lax
import numpy as np

D_MODEL = 2048
BATCH = 4
SEQ = 4096
DEPTH = 1

CHUNK = 64
Q_BLOCK = 128
N_HEADS = 8
HEAD_DIM = 64
ATT_WIDTH = N_HEADS * 2 * HEAD_DIM
CONV_CHANNELS = 1024
CONV_TAPS = 31
N_BUCKETS = 32
MAX_DISTANCE = 128
PEER_HEADS = 8
PEER_N_KEYS = 128
PEER_N_EXPERTS = PEER_N_KEYS * PEER_N_KEYS
PEER_HALF_DIM = 128
PEER_QUERY_DIM = 2 * PEER_HALF_DIM
PEER_TOPK = 16
PEER_TOKEN_BLOCK = 128
IN_COLS = 3 * ATT_WIDTH + 2 * CONV_CHANNELS + 2 * D_MODEL
EPS = 1e-6
NEG = -1e30

kernel_name = "hybrid_diffattn_conformer_peer_block"


def rms_norm(x, g):
    xf = x.astype(jnp.float32)
    y = xf * lax.rsqrt(jnp.mean(xf * xf, axis=-1, keepdims=True) + EPS)
    return (y * g.astype(jnp.float32)).astype(x.dtype)


def layer_norm(x, g, b):
    xf = x.astype(jnp.float32)
    mu = jnp.mean(xf, axis=-1, keepdims=True)
    xc = xf - mu
    var = jnp.mean(xc * xc, axis=-1, keepdims=True)
    y = xc * lax.rsqrt(var + EPS) * g.astype(jnp.float32) + b.astype(jnp.float32)
    return y.astype(x.dtype)


def rel_bucket(rel):
    nb = N_BUCKETS // 2
    max_exact = nb // 2
    ret = (rel > 0).astype(jnp.int32) * nb
    n = jnp.abs(rel)
    nf = jnp.maximum(n, 1).astype(jnp.float32)
    large = max_exact + (jnp.log(nf / max_exact) / math.log(MAX_DISTANCE / max_exact)
                         * (nb - max_exact)).astype(jnp.int32)
    large = jnp.minimum(large, nb - 1)
    return ret + jnp.where(n < max_exact, n, large)


def diff_attention(q, k, v, lam, rel_bias):
    S = q.shape[1]
    q = q * (HEAD_DIM ** -0.5)
    outs = []
    for i in range(S // Q_BLOCK):
        q0 = i * Q_BLOCK
        end = q0 + Q_BLOCK
        qb = q[:, q0:end]
        kb = k[:, :end]
        vb = v[:, :end]
        logits = jnp.einsum('bqhmd,bkhmd->bmhqk', qb, kb).astype(jnp.float32)
        q_pos = q0 + jnp.arange(Q_BLOCK, dtype=jnp.int32)
        k_pos = jnp.arange(end, dtype=jnp.int32)
        bias = rel_bias[rel_bucket(k_pos[None, :] - q_pos[:, None])]
        bias = jnp.transpose(bias, (2, 0, 1)).astype(jnp.float32)
        mask = (k_pos[None, :] // CHUNK) <= (q_pos[:, None] // CHUNK)
        logits = jnp.where(mask, logits + bias, NEG)
        p = jax.nn.softmax(logits, axis=-1)
        w = p[:, 0] - lam * p[:, 1]
        outs.append(jnp.einsum('bhqk,bkhe->bqhe', w.astype(v.dtype), vb))
    return jnp.concatenate(outs, axis=1)


def conformer_conv(glu_in, conv_w, conv_b, ln_g, ln_b, w_co):
    a, b = jnp.split(glu_in, 2, axis=-1)
    h = a * jax.nn.sigmoid(b)
    h = lax.conv_general_dilated(
        h, conv_w[:, None, :], window_strides=(1,), padding=[(CONV_TAPS - 1, 0)],
        dimension_numbers=('NWC', 'WIO', 'NWC'), feature_group_count=CONV_CHANNELS) + conv_b
    h = layer_norm(h, ln_g, ln_b)
    h = h * jax.nn.sigmoid(h)
    return h @ w_co


def peer(h, w_q, sub_keys, u_tab, v_tab):
    B, S, D = h.shape
    T = B * S
    ht = h.reshape(T, D)
    q = (ht @ w_q).reshape(T, PEER_HEADS, 2, PEER_HALF_DIM)
    scores = jnp.einsum('thpc,hpnc->thpn', q, sub_keys).astype(jnp.float32)
    s_half, i_half = lax.top_k(scores, PEER_TOPK)
    cand_s = s_half[:, :, 0, :, None] + s_half[:, :, 1, None, :]
    cand_i = i_half[:, :, 0, :, None] * PEER_N_KEYS + i_half[:, :, 1, None, :]
    n_cand = PEER_TOPK * PEER_TOPK
    top_s, pos = lax.top_k(cand_s.reshape(T, PEER_HEADS, n_cand), PEER_TOPK)
    idx = jnp.take_along_axis(cand_i.reshape(T, PEER_HEADS, n_cand), pos, axis=-1)
    g = jax.nn.softmax(top_s, axis=-1).astype(h.dtype)
    nb = T // PEER_TOKEN_BLOCK
    hk = PEER_HEADS * PEER_TOPK
    xs = (ht.reshape(nb, PEER_TOKEN_BLOCK, D),
          idx.reshape(nb, PEER_TOKEN_BLOCK, hk),
          g.reshape(nb, PEER_TOKEN_BLOCK, hk))

    def apply_block(args):
        xb, ib, gb = args
        a = jnp.einsum('tnd,td->tn', u_tab[ib], xb)
        wts = gb * jax.nn.gelu(a, approximate=False)
        return jnp.einsum('tn,tnd->td', wts, v_tab[ib])

    y = lax.map(apply_block, xs)
    return y.reshape(B, S, D)


def setup_inputs(seed: int = 0) -> dict:
    key = jax.random.key(seed)
    ks = jax.random.split(key, 24)

    def nrm(k, shape, scale):
        return jax.random.normal(k, shape, jnp.float32) * scale

    return {
        "x": nrm(ks[0], (BATCH, SEQ, D_MODEL), 1.0),
        "norm_mix_g": 1.0 + nrm(ks[1], (DEPTH, D_MODEL), 0.02),
        "w_in": nrm(ks[2], (DEPTH, D_MODEL, IN_COLS), D_MODEL ** -0.5),
        "b_gate": nrm(ks[3], (DEPTH, 2 * D_MODEL), 0.01),
        "lam_q1": nrm(ks[4], (DEPTH, HEAD_DIM), 0.1),
        "lam_k1": nrm(ks[5], (DEPTH, HEAD_DIM), 0.1),
        "lam_q2": nrm(ks[6], (DEPTH, HEAD_DIM), 0.1),
        "lam_k2": nrm(ks[7], (DEPTH, HEAD_DIM), 0.1),
        "subln_g": 1.0 + nrm(ks[8], (DEPTH, 2 * HEAD_DIM), 0.02),
        "w_att_out": nrm(ks[9], (DEPTH, ATT_WIDTH, D_MODEL), ATT_WIDTH ** -0.5),
        "conv_w": nrm(ks[10], (DEPTH, CONV_TAPS, CONV_CHANNELS), CONV_TAPS ** -0.5),
        "conv_b": nrm(ks[11], (DEPTH, CONV_CHANNELS), 0.01),
        "conv_ln_g": 1.0 + nrm(ks[12], (DEPTH, CONV_CHANNELS), 0.02),
        "conv_ln_b": nrm(ks[13], (DEPTH, CONV_CHANNELS), 0.01),
        "w_conv_out": nrm(ks[14], (DEPTH, CONV_CHANNELS, D_MODEL), CONV_CHANNELS ** -0.5),
        "w_out": nrm(ks[15], (DEPTH, D_MODEL, D_MODEL), D_MODEL ** -0.5),
        "rel_bias": nrm(ks[16], (N_BUCKETS, N_HEADS), 0.5),
        "norm_ffn_g": 1.0 + nrm(ks[17], (DEPTH, D_MODEL), 0.02),
        "peer_w_q": nrm(ks[18], (DEPTH, D_MODEL, PEER_HEADS * PEER_QUERY_DIM), D_MODEL ** -0.5),
        "peer_sub_keys": nrm(ks[19], (DEPTH, PEER_HEADS, 2, PEER_N_KEYS, PEER_HALF_DIM), PEER_HALF_DIM ** -0.5),
        "peer_u": nrm(ks[20], (DEPTH, PEER_N_EXPERTS, D_MODEL), D_MODEL ** -0.5),
        "peer_v": nrm(ks[21], (DEPTH, PEER_N_EXPERTS, D_MODEL), PEER_HEADS ** -0.5),
        "final_norm_g": 1.0 + nrm(ks[22], (D_MODEL,), 0.02),
    }


def reference(x, norm_mix_g, w_in, b_gate, lam_q1, lam_k1, lam_q2, lam_k2, subln_g,
              w_att_out, conv_w, conv_b, conv_ln_g, conv_ln_b, w_conv_out, w_out,
              rel_bias, norm_ffn_g, peer_w_q, peer_sub_keys, peer_u, peer_v, final_norm_g):
    B, S, _ = x.shape
    h = x
    cuts = [ATT_WIDTH, 2 * ATT_WIDTH, 3 * ATT_WIDTH, 3 * ATT_WIDTH + 2 * CONV_CHANNELS]
    for l in range(DEPTH):
        lam_init = 0.8 - 0.6 * math.exp(-0.3 * l)
        n = rms_norm(h, norm_mix_g[l])
        proj = n @ w_in[l]
        q, k, v, glu_in, gate_logits = jnp.split(proj, cuts, axis=-1)
        q = q.reshape(B, S, N_HEADS, 2, HEAD_DIM)
        k = k.reshape(B, S, N_HEADS, 2, HEAD_DIM)
        v = v.reshape(B, S, N_HEADS, 2 * HEAD_DIM)
        lam = (jnp.exp(jnp.sum(lam_q1[l].astype(jnp.float32) * lam_k1[l].astype(jnp.float32)))
               - jnp.exp(jnp.sum(lam_q2[l].astype(jnp.float32) * lam_k2[l].astype(jnp.float32)))
               + lam_init)
        o = diff_attention(q, k, v, lam, rel_bias)
        o = rms_norm(o, subln_g[l]) * (1.0 - lam_init)
        y_att = o.reshape(B, S, ATT_WIDTH) @ w_att_out[l]
        y_conv = conformer_conv(glu_in, conv_w[l], conv_b[l], conv_ln_g[l],
                                conv_ln_b[l], w_conv_out[l])
        g_att, g_conv = jnp.split(jax.nn.sigmoid(gate_logits + b_gate[l]), 2, axis=-1)
        h = h + (g_att * y_att + g_conv * y_conv) @ w_out[l]
        h = h + peer(rms_norm(h, norm_ffn_g[l]), peer_w_q[l], peer_sub_keys[l],
                     peer_u[l], peer_v[l])
    return rms_norm(h, final_norm_g)
```

```python
import functools
import math

import jax
import jax.numpy as jnp
from jax import lax
from jax.experimental import pallas as pl
from jax.experimental.pallas import tpu as pltpu

F32 = jnp.float32
BF16 = jnp.bfloat16

D_MODEL = 2048
N_HEADS = 8
HEAD_DIM = 64
HEAD_W = 2 * HEAD_DIM
ATT_WIDTH = N_HEADS * HEAD_W
CONV_CHANNELS = 1024
CONV_TAPS = 31
CHUNK = 64
N_BUCKETS = 32
MAX_DISTANCE = 128
PEER_HEADS = 8
PEER_N_KEYS = 128
PEER_TOPK = 16
EPS = 1e-6
NEG = -1e30
LAM_INIT = 0.8 - 0.6 * math.exp(-0.3 * 0)

LANES = 128
SUBLANES = 8
VMEM_LIMIT = 56 * 1024 * 1024


def _params(sem):
    return pltpu.CompilerParams(dimension_semantics=sem, vmem_limit_bytes=VMEM_LIMIT)


def _rms(x, g):
    ms = jnp.mean(x * x, axis=-1, keepdims=True)
    return x * lax.rsqrt(ms + EPS) * g


def _norm_once(x_ref, g_ref, n_ref):
    @pl.when(pl.program_id(1) == 0)
    def _():
        n_ref[...] = _rms(x_ref[...], g_ref[...]).astype(BF16)


def _proj_plain_kernel(x_ref, g_ref, w_ref, o_ref, n_ref):
    _norm_once(x_ref, g_ref, n_ref)
    o_ref[...] = jnp.dot(n_ref[...], w_ref[...], preferred_element_type=F32).astype(o_ref.dtype)


def _proj_glu_kernel(x_ref, g_ref, wa_ref, wb_ref, o_ref, n_ref):
    _norm_once(x_ref, g_ref, n_ref)
    n = n_ref[...]
    a = jnp.dot(n, wa_ref[...], preferred_element_type=F32)
    b = jnp.dot(n, wb_ref[...], preferred_element_type=F32)
    o_ref[...] = a * jax.nn.sigmoid(b)


def _proj_gate_kernel(x_ref, g_ref, w_ref, b_ref, o_ref, n_ref):
    _norm_once(x_ref, g_ref, n_ref)
    z = jnp.dot(n_ref[...], w_ref[...], preferred_element_type=F32) + b_ref[...]
    o_ref[...] = jax.nn.sigmoid(z)


def _norm_proj(body, x2, g, ws, extras, n_out, out_dtype, tm, tn):
    t, d = x2.shape
    grid = (t // tm, n_out // tn)
    in_specs = [pl.BlockSpec((tm, d), lambda i, j: (i, 0)),
                pl.BlockSpec((1, d), lambda i, j: (0, 0))]
    in_specs += [pl.BlockSpec((d, tn), lambda i, j: (0, j)) for _ in ws]
    in_specs += [pl.BlockSpec((1, tn), lambda i, j: (0, j)) for _ in extras]
    return pl.pallas_call(
        body,
        out_shape=jax.ShapeDtypeStruct((t, n_out), out_dtype),
        grid=grid,
        in_specs=in_specs,
        out_specs=pl.BlockSpec((tm, tn), lambda i, j: (i, j)),
        scratch_shapes=[pltpu.VMEM((tm, d), BF16)],
        compiler_params=_params(("parallel", "arbitrary")),
    )(x2, g, *ws, *extras)


def _rel_bucket(rel):
    nb = N_BUCKETS // 2
    max_exact = nb // 2
    ret = (rel > 0).astype(jnp.int32) * nb
    n = jnp.abs(rel)
    nf = jnp.maximum(n, 1).astype(F32)
    large = max_exact + (jnp.log(nf / max_exact) / math.log(MAX_DISTANCE / max_exact)
                         * (nb - max_exact)).astype(jnp.int32)
    large = jnp.minimum(large, nb - 1)
    return ret + jnp.where(n < max_exact, n, large)


def _bias_tiles(rel_bias, tq, tk):
    far = rel_bias[N_BUCKETS // 2 - 1]
    qpos = jnp.arange(tq, dtype=jnp.int32)[:, None]
    kpos = jnp.arange(tk, dtype=jnp.int32)[None, :]
    diag = rel_bias[_rel_bucket(kpos - qpos)] - far
    sub = rel_bias[_rel_bucket(kpos - tk - qpos)] - far
    return (jnp.transpose(diag, (2, 0, 1)).astype(F32), jnp.transpose(sub, (2, 0, 1)).astype(F32))


def _attn_kernel(lam_ref, q_ref, k_ref, v_ref, bd_ref, bs_ref, g_ref, o_ref, m_ref, l_ref, acc_ref,
                 *, tq, tk):
    qi = pl.program_id(1)
    q = q_ref[...]
    lane = lax.broadcasted_iota(jnp.int32, q.shape, 1)
    zero = jnp.zeros_like(q)
    qm = (jnp.where(lane < HEAD_DIM, q, zero), jnp.where(lane >= HEAD_DIM, q, zero))

    m_ref[...] = jnp.full(m_ref.shape, NEG, F32)
    l_ref[...] = jnp.zeros(l_ref.shape, F32)
    acc_ref[...] = jnp.zeros(acc_ref.shape, F32)

    def update(kt, vt, fix):
        for mp in range(2):
            s = lax.dot_general(qm[mp], kt, (((1,), (1,)), ((), ())), preferred_element_type=F32)
            s = fix(s)
            m_old = m_ref[mp]
            m_new = jnp.maximum(m_old, jnp.max(s, axis=-1, keepdims=True))
            alpha = jnp.exp(m_old - m_new)
            p = jnp.exp(s - m_new)
            l_ref[mp] = alpha * l_ref[mp] + jnp.sum(p, axis=-1, keepdims=True)
            acc_ref[mp] = alpha * acc_ref[mp] + jnp.dot(p.astype(BF16), vt, preferred_element_type=F32)
            m_ref[mp] = m_new

    def far_body(j, carry):
        off = pl.multiple_of(j * tk, tk)
        update(k_ref[pl.ds(off, tk), :], v_ref[pl.ds(off, tk), :], lambda s: s)
        return carry

    lax.fori_loop(0, qi - 1, far_body, 0)

    @pl.when(qi >= 1)
    def _():
        off = pl.multiple_of((qi - 1) * tk, tk)
        update(k_ref[pl.ds(off, tk), :], v_ref[pl.ds(off, tk), :], lambda s: s + bs_ref[0])

    off = pl.multiple_of(qi * tk, tk)
    qc = lax.broadcasted_iota(jnp.int32, (tq, tk), 0) // CHUNK
    kc = lax.broadcasted_iota(jnp.int32, (tq, tk), 1) // CHUNK
    visible = kc <= qc
    update(k_ref[pl.ds(off, tk), :], v_ref[pl.ds(off, tk), :],
           lambda s: jnp.where(visible, s + bd_ref[0], NEG))

    lam = lam_ref[0]
    o = acc_ref[0] / l_ref[0] - lam * (acc_ref[1] / l_ref[1])
    o = _rms(o, g_ref[...]) * (1.0 - LAM_INIT)
    o_ref[...] = o.astype(o_ref.dtype)


def _diff_attention(qkv, lam, bias_diag, bias_sub, subln_g, batch, seq, tq):
    tk = tq
    nq = seq // tq
    kern = functools.partial(_attn_kernel, tq=tq, tk=tk)
    return pl.pallas_call(
        kern,
        out_shape=jax.ShapeDtypeStruct((batch * seq, ATT_WIDTH), BF16),
        grid=(batch * N_HEADS, nq),
        in_specs=[
            pl.BlockSpec(memory_space=pltpu.SMEM),
            pl.BlockSpec((tq, HEAD_W), lambda bh, i: ((bh // N_HEADS) * nq + i, bh % N_HEADS)),
            pl.BlockSpec((seq, HEAD_W), lambda bh, i: (bh // N_HEADS, N_HEADS + bh % N_HEADS)),
            pl.BlockSpec((seq, HEAD_W), lambda bh, i: (bh // N_HEADS, 2 * N_HEADS + bh % N_HEADS)),
            pl.BlockSpec((1, tq, tk), lambda bh, i: (bh % N_HEADS, 0, 0)),
            pl.BlockSpec((1, tq, tk), lambda bh, i: (bh % N_HEADS, 0, 0)),
            pl.BlockSpec((1, HEAD_W), lambda bh, i: (0, 0)),
        ],
        out_specs=pl.BlockSpec((tq, HEAD_W), lambda bh, i: ((bh // N_HEADS) * nq + i, bh % N_HEADS)),
        scratch_shapes=[pltpu.VMEM((2, tq, 1), F32), pltpu.VMEM((2, tq, 1), F32),
                        pltpu.VMEM((2, tq, HEAD_W), F32)],
        compiler_params=_params(("parallel", "arbitrary")),
    )(lam, qkv, qkv, qkv, bias_diag, bias_sub, subln_g)


CONV_HALO = 32
CONV_ROWS = 32


def _conv_kernel(cur_ref, halo_ref, w_ref, b_ref, lg_ref, lb_ref, o_ref, ext_ref, y_ref,
                 *, ts, tiles_per_seq):
    i = pl.program_id(0)
    first = (i % tiles_per_seq) == 0
    halo = halo_ref[...]
    ext_ref[0:CONV_HALO, :] = jnp.where(first, jnp.zeros_like(halo), halo)
    ext_ref[CONV_HALO:, :] = cur_ref[...]
    lead = CONV_HALO - (CONV_TAPS - 1)
    nsub = CONV_ROWS // SUBLANES

    for c in range(CONV_CHANNELS // LANES):
        lanes = slice(c * LANES, (c + 1) * LANES)
        wb = [jnp.broadcast_to(w_ref[j:j + 1, lanes], (SUBLANES, LANES)) for j in range(CONV_TAPS)]
        bb = jnp.broadcast_to(b_ref[:, lanes], (SUBLANES, LANES))

        def row_body(r, carry, lanes=lanes, wb=wb, bb=bb):
            r0 = pl.multiple_of(r * CONV_ROWS, CONV_ROWS)
            win_rows = CONV_ROWS + CONV_HALO
            win = ext_ref[pl.ds(r0, win_rows), lanes]
            acc = [bb] * nsub
            for sh in range(SUBLANES):
                rolled = win if sh == 0 else pltpu.roll(win, win_rows - sh, axis=0)
                for a in range(win_rows // SUBLANES):
                    j = a * SUBLANES + sh - lead
                    if 0 <= j < CONV_TAPS:
                        for u in range(nsub):
                            lo = (a + u) * SUBLANES
                            acc[u] = acc[u] + rolled[lo:lo + SUBLANES] * wb[j]
            for u in range(nsub):
                y_ref[pl.ds(r0 + u * SUBLANES, SUBLANES), lanes] = acc[u]
            return carry

        lax.fori_loop(0, ts // CONV_ROWS, row_body, 0)

    y = y_ref[...]
    mu = jnp.mean(y, axis=-1, keepdims=True)
    yc = y - mu
    var = jnp.mean(yc * yc, axis=-1, keepdims=True)
    z = yc * lax.rsqrt(var + EPS) * lg_ref[...] + lb_ref[...]
    o_ref[...] = (z * jax.nn.sigmoid(z)).astype(o_ref.dtype)


def _conv_module(hglu, conv_w, conv_b, ln_g, ln_b, seq, ts):
    t, c = hglu.shape
    kern = functools.partial(_conv_kernel, ts=ts, tiles_per_seq=seq // ts)
    hb = ts // CONV_HALO
    return pl.pallas_call(
        kern,
        out_shape=jax.ShapeDtypeStruct((t, c), BF16),
        grid=(t // ts,),
        in_specs=[
            pl.BlockSpec((ts, c), lambda i: (i, 0)),
            pl.BlockSpec((CONV_HALO, c), lambda i: (jnp.maximum(i * hb - 1, 0), 0)),
            pl.BlockSpec((CONV_TAPS, c), lambda i: (0, 0)),
            pl.BlockSpec((1, c), lambda i: (0, 0)),
            pl.BlockSpec((1, c), lambda i: (0, 0)),
            pl.BlockSpec((1, c), lambda i: (0, 0)),
        ],
        out_specs=pl.BlockSpec((ts, c), lambda i: (i, 0)),
        scratch_shapes=[pltpu.VMEM((ts + CONV_HALO, c), F32), pltpu.VMEM((ts, c), F32)],
        compiler_params=_params(("parallel",)),
    )(hglu, hglu, conv_w, conv_b, ln_g, ln_b)


def _mix_kernel(o_ref, hc_ref, ga_ref, gc_ref, x_ref, wa_ref, wc_ref, wo_ref, h_ref):
    ya = jnp.dot(o_ref[...], wa_ref[...], preferred_element_type=F32)
    yc = jnp.dot(hc_ref[...], wc_ref[...], preferred_element_type=F32)
    mix = (ga_ref[...] * ya + gc_ref[...] * yc).astype(BF16)
    h_ref[...] = x_ref[...] + jnp.dot(mix, wo_ref[...], preferred_element_type=F32)


def _const_spec(shape):
    nd = len(shape)
    return pl.BlockSpec(shape, lambda *_: (0,) * nd, pipeline_mode=pl.Buffered(1))


def _mix(o, hc, gates, x2, w_att, w_conv, w_out, tm):
    t, d = x2.shape
    return pl.pallas_call(
        _mix_kernel,
        out_shape=jax.ShapeDtypeStruct((t, d), F32),
        grid=(t // tm,),
        in_specs=[
            pl.BlockSpec((tm, ATT_WIDTH), lambda i: (i, 0)),
            pl.BlockSpec((tm, CONV_CHANNELS), lambda i: (i, 0)),
            pl.BlockSpec((tm, d), lambda i: (i, 0)),
            pl.BlockSpec((tm, d), lambda i: (i, 1)),
            pl.BlockSpec((tm, d), lambda i: (i, 0)),
            _const_spec(w_att.shape), _const_spec(w_conv.shape), _const_spec(w_out.shape),
        ],
        out_specs=pl.BlockSpec((tm, d), lambda i: (i, 0)),
        compiler_params=_params(("parallel",)),
    )(o, hc, gates, gates, x2, w_att, w_conv, w_out)


def _norm_t_kernel(h_ref, g_ref, o_ref):
    o_ref[...] = _rms(h_ref[...], g_ref[...]).T.astype(BF16)


def _norm_transpose(h, g, tt):
    t, d = h.shape
    return pl.pallas_call(
        _norm_t_kernel,
        out_shape=jax.ShapeDtypeStruct((d, t), BF16),
        grid=(t // tt,),
        in_specs=[pl.BlockSpec((tt, d), lambda i: (i, 0)), pl.BlockSpec((1, d), lambda i: (0, 0))],
        out_specs=pl.BlockSpec((d, tt), lambda i: (0, i)),
        compiler_params=_params(("parallel",)),
    )(h, g)


def _scores_kernel(xt_ref, wq_ref, keys_ref, s_ref):
    qt = jnp.dot(wq_ref[...], xt_ref[...], preferred_element_type=F32).astype(BF16)
    for hp in range(2 * PEER_HEADS):
        rows = slice(hp * PEER_N_KEYS, (hp + 1) * PEER_N_KEYS)
        s_ref[rows, :] = jnp.dot(keys_ref[hp], qt[rows, :], preferred_element_type=F32)


def _peer_scores(xt, wq_t, keys, tt):
    d, t = xt.shape
    return pl.pallas_call(
        _scores_kernel,
        out_shape=jax.ShapeDtypeStruct((d, t), F32),
        grid=(t // tt,),
        in_specs=[pl.BlockSpec((d, tt), lambda i: (0, i)), _const_spec(wq_t.shape), _const_spec(keys.shape)],
        out_specs=pl.BlockSpec((d, tt), lambda i: (0, i)),
        compiler_params=_params(("parallel",)),
    )(xt, wq_t, keys)


TOPN = PEER_TOPK + 1


def _cand_pairs():
    return [(k, l) for k in range(TOPN) for l in range(TOPN) if (k + 1) * (l + 1) <= TOPN]


CAND_PAIRS = _cand_pairs()
CAND_ROWS = -(-len(CAND_PAIRS) // SUBLANES) * SUBLANES


def _top_values(x, n):
    out = []
    for _ in range(n):
        m = jnp.max(x, axis=0, keepdims=True)
        out.append(m)
        x = jnp.where(x == m, -jnp.inf, x)
    return out


def _topk_kernel(s_ref, c_ref, d_ref, e1_ref, cand_ref):
    for h in range(PEER_HEADS):
        s0 = s_ref[h, 0]
        s1 = s_ref[h, 1]
        a = _top_values(s0, TOPN)
        b = _top_values(s1, TOPN)
        cand_ref[...] = jnp.full(cand_ref.shape, -jnp.inf, F32)
        for r, (k, l) in enumerate(CAND_PAIRS):
            cand_ref[r:r + 1, :] = a[k] + b[l]
        cand = cand_ref[...]
        tops = _top_values(cand, TOPN)
        thr = 0.5 * (tops[PEER_TOPK - 1] + tops[PEER_TOPK])
        mx = a[0] + b[0]
        z = jnp.sum(jnp.where(cand >= thr, jnp.exp(cand - mx), 0.0), axis=0, keepdims=True)
        c_ref[h] = thr - s0
        d_ref[h] = jnp.exp(s0 - a[0]) / z
        e1_ref[h] = jnp.exp(s1 - b[0])


def _peer_topk(scores4, tt):
    hh, _, nk, t = scores4.shape
    out = jax.ShapeDtypeStruct((hh, nk, t), F32)
    spec = pl.BlockSpec((hh, nk, tt), lambda i: (0, 0, i))
    return pl.pallas_call(
        _topk_kernel,
        out_shape=(out, out, out),
        grid=(t // tt,),
        in_specs=[pl.BlockSpec((hh, 2, nk, tt), lambda i: (0, 0, 0, i))],
        out_specs=(spec, spec, spec),
        scratch_shapes=[pltpu.VMEM((CAND_ROWS, tt), F32)],
        compiler_params=_params(("parallel",)),
    )(scores4)


PEER_ROWS = 16


def _peer_dense_kernel(xt_ref, u_ref, vt_ref, c_ref, d_ref, s1_ref, e1_ref, yt_ref, at_ref, wt_ref,
                       *, tt, ni):
    e = pl.program_id(1)

    @pl.when(e == 0)
    def _():
        yt_ref[...] = jnp.zeros(yt_ref.shape, F32)

    at_ref[...] = jnp.dot(u_ref[...], xt_ref[...], preferred_element_type=F32)

    def lane_body(lg, carry):
        lanes = pl.ds(pl.multiple_of(lg * LANES, LANES), LANES)
        for r in range(ni):
            cb = [jnp.broadcast_to(c_ref[h, r:r + 1, lanes], (PEER_ROWS, LANES)) for h in range(PEER_HEADS)]
            db = [jnp.broadcast_to(d_ref[h, r:r + 1, lanes], (PEER_ROWS, LANES)) for h in range(PEER_HEADS)]
            for jp in range(PEER_N_KEYS // PEER_ROWS):
                rows = slice(jp * PEER_ROWS, (jp + 1) * PEER_ROWS)
                g = jnp.zeros((PEER_ROWS, LANES), F32)
                for h in range(PEER_HEADS):
                    g = g + jnp.where(s1_ref[h, rows, lanes] >= cb[h], e1_ref[h, rows, lanes] * db[h], 0.0)
                arow = slice(r * PEER_N_KEYS + jp * PEER_ROWS, r * PEER_N_KEYS + (jp + 1) * PEER_ROWS)
                a = at_ref[arow, lanes]
                w = 0.5 * a * (1.0 + lax.erf(a * (2.0 ** -0.5))) * g
                wt_ref[arow, lanes] = w.astype(BF16)
        return carry

    lax.fori_loop(0, tt // LANES, lane_body, 0)
    yt_ref[...] += jnp.dot(vt_ref[...], wt_ref[...], preferred_element_type=F32)


def _peer_dense(xt, u, vt, c, d, scores4, e1, tt, ni):
    dm, t = xt.shape
    n_exp = u.shape[0]
    et = ni * PEER_N_KEYS
    hh, nk, _ = c.shape
    kern = functools.partial(_peer_dense_kernel, tt=tt, ni=ni)
    tok_spec = pl.BlockSpec((hh, nk, tt), lambda i, e: (0, 0, i), pipeline_mode=pl.Buffered(1))
    return pl.pallas_call(
        kern,
        out_shape=jax.ShapeDtypeStruct((dm, t), F32),
        grid=(t // tt, n_exp // et),
        in_specs=[
            pl.BlockSpec((dm, tt), lambda i, e: (0, i)),
            pl.BlockSpec((et, dm), lambda i, e: (e, 0)),
            pl.BlockSpec((dm, et), lambda i, e: (0, e)),
            pl.BlockSpec((hh, ni, tt), lambda i, e: (0, e, i)),
            pl.BlockSpec((hh, ni, tt), lambda i, e: (0, e, i)),
            pl.BlockSpec((hh, None, nk, tt), lambda i, e: (0, 1, 0, i), pipeline_mode=pl.Buffered(1)),
            tok_spec,
        ],
        out_specs=pl.BlockSpec((dm, tt), lambda i, e: (0, i)),
        scratch_shapes=[pltpu.VMEM((et, tt), F32), pltpu.VMEM((et, tt), BF16)],
        compiler_params=_params(("parallel", "arbitrary")),
    )(xt, u, vt, c, d, scores4, e1)


def _final_kernel(h_ref, yt_ref, g_ref, o_ref):
    o_ref[...] = _rms(h_ref[...] + yt_ref[...].T, g_ref[...])


def _final(h, yt, g, tt):
    t, d = h.shape
    return pl.pallas_call(
        _final_kernel,
        out_shape=jax.ShapeDtypeStruct((t, d), F32),
        grid=(t // tt,),
        in_specs=[pl.BlockSpec((tt, d), lambda i: (i, 0)), pl.BlockSpec((d, tt), lambda i: (0, i)),
                  pl.BlockSpec((1, d), lambda i: (0, 0))],
        out_specs=pl.BlockSpec((tt, d), lambda i: (i, 0)),
        compiler_params=_params(("parallel",)),
    )(h, yt, g)


def kernel(x, norm_mix_g, w_in, b_gate, lam_q1, lam_k1, lam_q2, lam_k2, subln_g, w_att_out, conv_w,
           conv_b, conv_ln_g, conv_ln_b, w_conv_out, w_out, rel_bias, norm_ffn_g, peer_w_q,
           peer_sub_keys, peer_u, peer_v, final_norm_g):
    batch, seq, d = x.shape
    t = batch * seq
    depth = w_in.shape[0]
    assert depth == 1 and d == D_MODEL
    h = x.reshape(t, d)

    c_q, c_k, c_v = ATT_WIDTH, 2 * ATT_WIDTH, 3 * ATT_WIDTH
    c_glu = c_v + 2 * CONV_CHANNELS
    tq = 512
    bias_diag, bias_sub = _bias_tiles(rel_bias, tq, tq)

    l = 0
    w = w_in[l]
    w_qkv = jnp.concatenate([w[:, :c_q] * (HEAD_DIM ** -0.5), w[:, c_q:c_v]], axis=1).astype(BF16)
    w_glu_a = w[:, c_v:c_v + CONV_CHANNELS].astype(BF16)
    w_glu_b = w[:, c_v + CONV_CHANNELS:c_glu].astype(BF16)
    w_gate = w[:, c_glu:].astype(BF16)
    g_mix = norm_mix_g[l].reshape(1, d)

    qkv = _norm_proj(_proj_plain_kernel, h, g_mix, [w_qkv], [], 3 * ATT_WIDTH, BF16, 1024, 512)
    hglu = _norm_proj(_proj_glu_kernel, h, g_mix, [w_glu_a, w_glu_b], [], CONV_CHANNELS, F32, 1024, 512)
    gates = _norm_proj(_proj_gate_kernel, h, g_mix, [w_gate], [b_gate[l].reshape(1, -1)], 2 * d, F32, 1024, 512)

    lam = (jnp.exp(jnp.sum(lam_q1[l].astype(F32) * lam_k1[l].astype(F32)))
           - jnp.exp(jnp.sum(lam_q2[l].astype(F32) * lam_k2[l].astype(F32))) + LAM_INIT).reshape(1)
    o = _diff_attention(qkv, lam, bias_diag, bias_sub, subln_g[l].reshape(1, HEAD_W), batch, seq, tq)

    hc = _conv_module(hglu, conv_w[l], conv_b[l].reshape(1, -1), conv_ln_g[l].reshape(1, -1),
                      conv_ln_b[l].reshape(1, -1), seq, 256)

    h1 = _mix(o, hc, gates, h, w_att_out[l].astype(BF16), w_conv_out[l].astype(BF16),
              w_out[l].astype(BF16), 256)

    xt = _norm_transpose(h1, norm_ffn_g[l].reshape(1, d), 512)
    keys = peer_sub_keys[l].reshape(2 * PEER_HEADS, PEER_N_KEYS, -1).astype(BF16)
    scores = _peer_scores(xt, peer_w_q[l].T.astype(BF16), keys, 512)
    scores4 = scores.reshape(PEER_HEADS, 2, PEER_N_KEYS, t)
    c, dd, e1 = _peer_topk(scores4, 512)
    yt = _peer_dense(xt, peer_u[l].astype(BF16), peer_v[l].T.astype(BF16), c, dd, scores4, e1, 512, 8)
    out = _final(h1, yt, final_norm_g.reshape(1, d), 512)
    return out.reshape(batch, seq, d)
```

```python
import functools
import math

import jax
import jax.numpy as jnp
from jax import lax
from jax.experimental import pallas as pl
from jax.experimental.pallas import tpu as pltpu

F32 = jnp.float32
BF16 = jnp.bfloat16

D_MODEL = 2048
N_HEADS = 8
HEAD_DIM = 64
HEAD_W = 2 * HEAD_DIM
ATT_WIDTH = N_HEADS * HEAD_W
CONV_CHANNELS = 1024
CONV_TAPS = 31
CHUNK = 64
N_BUCKETS = 32
MAX_DISTANCE = 128
PEER_HEADS = 8
PEER_N_KEYS = 128
PEER_TOPK = 16
EPS = 1e-6
NEG = -1e30
LAM_INIT = 0.8 - 0.6 * math.exp(-0.3 * 0)

LANES = 128
SUBLANES = 8
VMEM_LIMIT = 56 * 1024 * 1024


def _params(sem):
    return pltpu.CompilerParams(dimension_semantics=sem, vmem_limit_bytes=VMEM_LIMIT)


def _rms(x, g):
    ms = jnp.mean(x * x, axis=-1, keepdims=True)
    return x * lax.rsqrt(ms + EPS) * g


def _norm_once(x_ref, g_ref, n_ref):
    @pl.when(pl.program_id(1) == 0)
    def _():
        n_ref[...] = _rms(x_ref[...], g_ref[...]).astype(BF16)


def _proj_plain_kernel(x_ref, g_ref, w_ref, o_ref, n_ref):
    _norm_once(x_ref, g_ref, n_ref)
    o_ref[...] = jnp.dot(n_ref[...], w_ref[...], preferred_element_type=F32).astype(o_ref.dtype)


def _proj_glu_kernel(x_ref, g_ref, wa_ref, wb_ref, o_ref, n_ref):
    _norm_once(x_ref, g_ref, n_ref)
    n = n_ref[...]
    a = jnp.dot(n, wa_ref[...], preferred_element_type=F32)
    b = jnp.dot(n, wb_ref[...], preferred_element_type=F32)
    o_ref[...] = a * jax.nn.sigmoid(b)


def _proj_gate_kernel(x_ref, g_ref, w_ref, b_ref, o_ref, n_ref):
    _norm_once(x_ref, g_ref, n_ref)
    z = jnp.dot(n_ref[...], w_ref[...], preferred_element_type=F32) + b_ref[...]
    o_ref[...] = jax.nn.sigmoid(z)


def _norm_proj(name, body, x2, g, ws, extras, n_out, out_dtype, tm, tn):
    t, d = x2.shape
    grid = (t // tm, n_out // tn)
    in_specs = [pl.BlockSpec((tm, d), lambda i, j: (i, 0)),
                pl.BlockSpec((1, d), lambda i, j: (0, 0))]
    in_specs += [pl.BlockSpec((d, tn), lambda i, j: (0, j)) for _ in ws]
    in_specs += [pl.BlockSpec((1, tn), lambda i, j: (0, j)) for _ in extras]
    return pl.pallas_call(
        body,
        out_shape=jax.ShapeDtypeStruct((t, n_out), out_dtype),
        grid=grid,
        in_specs=in_specs,
        out_specs=pl.BlockSpec((tm, tn), lambda i, j: (i, j)),
        scratch_shapes=[pltpu.VMEM((tm, d), BF16)],
        compiler_params=_params(("parallel", "arbitrary")),
        name=name,
    )(x2, g, *ws, *extras)


def _rel_bucket(rel):
    nb = N_BUCKETS // 2
    max_exact = nb // 2
    ret = (rel > 0).astype(jnp.int32) * nb
    n = jnp.abs(rel)
    nf = jnp.maximum(n, 1).astype(F32)
    large = max_exact + (jnp.log(nf / max_exact) / math.log(MAX_DISTANCE / max_exact)
                         * (nb - max_exact)).astype(jnp.int32)
    large = jnp.minimum(large, nb - 1)
    return ret + jnp.where(n < max_exact, n, large)


def _bias_tiles(rel_bias, tq, tk):
    far = rel_bias[N_BUCKETS // 2 - 1]
    span = tq + tk - 1

    def expand(key_offset):
        rel = jnp.arange(span, dtype=jnp.int32) - (tq - 1) + key_offset
        table = (rel_bias[_rel_bucket(rel)] - far).T.astype(F32)
        rolled = jnp.roll(table, -(tq - 1), axis=1)
        flat = jnp.tile(rolled, (1, tq))[:, :tq * (span - 1)]
        tile_qk = flat.reshape(-1, tq, span - 1)[:, :, :tk]
        return jnp.transpose(tile_qk, (0, 2, 1))

    return expand(0), expand(-tk)


def _attn_kernel(lam_ref, q_ref, k_ref, v_ref, bd_ref, bs_ref, g_ref, o_ref, vt_ref, m_ref, l_ref, acc_ref,
                 *, tq, tk):
    qi = pl.program_id(1)

    @pl.when(qi == 0)
    def _():
        for j in range(vt_ref.shape[0]):
            vt_ref[j] = v_ref[j * tk:(j + 1) * tk, :].T

    q = q_ref[...]
    lane = lax.broadcasted_iota(jnp.int32, q.shape, 1)
    zero = jnp.zeros_like(q)
    qm = (jnp.where(lane < HEAD_DIM, q, zero), jnp.where(lane >= HEAD_DIM, q, zero))

    m_ref[...] = jnp.full(m_ref.shape, NEG, F32)
    l_ref[...] = jnp.zeros(l_ref.shape, F32)
    acc_ref[...] = jnp.zeros(acc_ref.shape, F32)

    def update(j, fix):
        kt = k_ref[pl.ds(pl.multiple_of(j * tk, tk), tk), :]
        vt = vt_ref[j]
        for mp in range(2):
            s = lax.dot_general(kt, qm[mp], (((1,), (1,)), ((), ())), preferred_element_type=F32)
            s = fix(s)
            m_old = m_ref[mp]
            m_new = jnp.maximum(m_old, jnp.max(s, axis=0, keepdims=True))
            alpha = jnp.exp(m_old - m_new)
            p = jnp.exp(s - m_new)
            l_ref[mp] = alpha * l_ref[mp] + jnp.sum(p, axis=0, keepdims=True)
            acc_ref[mp] = alpha * acc_ref[mp] + jnp.dot(vt, p.astype(BF16), preferred_element_type=F32)
            m_ref[mp] = m_new

    def far_body(j, carry):
        update(j, lambda s: s)
        return carry

    lax.fori_loop(0, qi - 1, far_body, 0)

    @pl.when(qi >= 1)
    def _():
        update(qi - 1, lambda s: s + bs_ref[0])

    kc = lax.broadcasted_iota(jnp.int32, (tk, tq), 0) // CHUNK
    qc = lax.broadcasted_iota(jnp.int32, (tk, tq), 1) // CHUNK
    visible = kc <= qc
    update(qi, lambda s: jnp.where(visible, s + bd_ref[0], NEG))

    lam = lam_ref[0]
    ot = acc_ref[0] / l_ref[0] - lam * (acc_ref[1] / l_ref[1])
    ms = jnp.mean(ot * ot, axis=0, keepdims=True)
    ot = ot * lax.rsqrt(ms + EPS) * g_ref[...] * (1.0 - LAM_INIT)
    o_ref[...] = ot.T.astype(o_ref.dtype)


def _diff_attention(qkv, lam, bias_diag, bias_sub, subln_g, batch, seq, tq):
    tk = tq
    nq = seq // tq
    kern = functools.partial(_attn_kernel, tq=tq, tk=tk)
    return pl.pallas_call(
        kern,
        out_shape=jax.ShapeDtypeStruct((batch * seq, ATT_WIDTH), BF16),
        grid=(batch * N_HEADS, nq),
        in_specs=[
            pl.BlockSpec(memory_space=pltpu.SMEM),
            pl.BlockSpec((tq, HEAD_W), lambda bh, i: ((bh // N_HEADS) * nq + i, bh % N_HEADS)),
            pl.BlockSpec((seq, HEAD_W), lambda bh, i: (bh // N_HEADS, N_HEADS + bh % N_HEADS)),
            pl.BlockSpec((seq, HEAD_W), lambda bh, i: (bh // N_HEADS, 2 * N_HEADS + bh % N_HEADS)),
            pl.BlockSpec((1, tk, tq), lambda bh, i: (bh % N_HEADS, 0, 0)),
            pl.BlockSpec((1, tk, tq), lambda bh, i: (bh % N_HEADS, 0, 0)),
            pl.BlockSpec((HEAD_W, 1), lambda bh, i: (0, 0)),
        ],
        out_specs=pl.BlockSpec((tq, HEAD_W), lambda bh, i: ((bh // N_HEADS) * nq + i, bh % N_HEADS)),
        scratch_shapes=[pltpu.VMEM((seq // tk, HEAD_W, tk), BF16),
                        pltpu.VMEM((2, 1, tq), F32), pltpu.VMEM((2, 1, tq), F32),
                        pltpu.VMEM((2, HEAD_W, tq), F32)],
        compiler_params=_params(("parallel", "arbitrary")),
        name="diff_attention",
    )(lam, qkv, qkv, qkv, bias_diag, bias_sub, subln_g)


CONV_HALO = 32
CONV_ROWS = 32


def _conv_kernel(cur_ref, halo_ref, w_ref, b_ref, lg_ref, lb_ref, o_ref, ext_ref, y_ref,
                 *, ts, tiles_per_seq):
    i = pl.program_id(0)
    first = (i % tiles_per_seq) == 0
    halo = halo_ref[...]
    ext_ref[0:CONV_HALO, :] = jnp.where(first, jnp.zeros_like(halo), halo)
    ext_ref[CONV_HALO:, :] = cur_ref[...]
    lead = CONV_HALO - (CONV_TAPS - 1)
    nsub = CONV_ROWS // SUBLANES

    for c in range(CONV_CHANNELS // LANES):
        lanes = slice(c * LANES, (c + 1) * LANES)
        wb = [jnp.broadcast_to(w_ref[j:j + 1, lanes], (SUBLANES, LANES)) for j in range(CONV_TAPS)]
        bb = jnp.broadcast_to(b_ref[:, lanes], (SUBLANES, LANES))

        def row_body(r, carry, lanes=lanes, wb=wb, bb=bb):
            r0 = pl.multiple_of(r * CONV_ROWS, CONV_ROWS)
            win_rows = CONV_ROWS + CONV_HALO
            win = ext_ref[pl.ds(r0, win_rows), lanes]
            acc = [bb] * nsub
            for sh in range(SUBLANES):
                rolled = win if sh == 0 else pltpu.roll(win, win_rows - sh, axis=0)
                for a in range(win_rows // SUBLANES):
                    j = a * SUBLANES + sh - lead
                    if 0 <= j < CONV_TAPS:
                        for u in range(nsub):
                            lo = (a + u) * SUBLANES
                            acc[u] = acc[u] + rolled[lo:lo + SUBLANES] * wb[j]
            for u in range(nsub):
                y_ref[pl.ds(r0 + u * SUBLANES, SUBLANES), lanes] = acc[u]
            return carry

        lax.fori_loop(0, ts // CONV_ROWS, row_body, 0)

    y = y_ref[...]
    mu = jnp.mean(y, axis=-1, keepdims=True)
    yc = y - mu
    var = jnp.mean(yc * yc, axis=-1, keepdims=True)
    z = yc * lax.rsqrt(var + EPS) * lg_ref[...] + lb_ref[...]
    o_ref[...] = (z * jax.nn.sigmoid(z)).astype(o_ref.dtype)


def _conv_module(hglu, conv_w, conv_b, ln_g, ln_b, seq, ts):
    t, c = hglu.shape
    kern = functools.partial(_conv_kernel, ts=ts, tiles_per_seq=seq // ts)
    hb = ts // CONV_HALO
    return pl.pallas_call(
        kern,
        out_shape=jax.ShapeDtypeStruct((t, c), BF16),
        grid=(t // ts,),
        in_specs=[
            pl.BlockSpec((ts, c), lambda i: (i, 0)),
            pl.BlockSpec((CONV_HALO, c), lambda i: (jnp.maximum(i * hb - 1, 0), 0)),
            pl.BlockSpec((CONV_TAPS, c), lambda i: (0, 0)),
            pl.BlockSpec((1, c), lambda i: (0, 0)),
            pl.BlockSpec((1, c), lambda i: (0, 0)),
            pl.BlockSpec((1, c), lambda i: (0, 0)),
        ],
        out_specs=pl.BlockSpec((ts, c), lambda i: (i, 0)),
        scratch_shapes=[pltpu.VMEM((ts + CONV_HALO, c), F32), pltpu.VMEM((ts, c), F32)],
        compiler_params=_params(("parallel",)),
        name="conv_module",
    )(hglu, hglu, conv_w, conv_b, ln_g, ln_b)


def _mix_kernel(o_ref, hc_ref, ga_ref, gc_ref, x_ref, wa_ref, wc_ref, wo_ref, h_ref):
    ya = jnp.dot(o_ref[...], wa_ref[...], preferred_element_type=F32)
    yc = jnp.dot(hc_ref[...], wc_ref[...], preferred_element_type=F32)
    mix = (ga_ref[...] * ya + gc_ref[...] * yc).astype(BF16)
    h_ref[...] = x_ref[...] + jnp.dot(mix, wo_ref[...], preferred_element_type=F32)


def _const_spec(shape):
    nd = len(shape)
    return pl.BlockSpec(shape, lambda *_: (0,) * nd, pipeline_mode=pl.Buffered(1))


def _mix(o, hc, gates, x2, w_att, w_conv, w_out, tm):
    t, d = x2.shape
    return pl.pallas_call(
        _mix_kernel,
        out_shape=jax.ShapeDtypeStruct((t, d), F32),
        grid=(t // tm,),
        in_specs=[
            pl.BlockSpec((tm, ATT_WIDTH), lambda i: (i, 0)),
            pl.BlockSpec((tm, CONV_CHANNELS), lambda i: (i, 0)),
            pl.BlockSpec((tm, d), lambda i: (i, 0)),
            pl.BlockSpec((tm, d), lambda i: (i, 1)),
            pl.BlockSpec((tm, d), lambda i: (i, 0)),
            _const_spec(w_att.shape), _const_spec(w_conv.shape), _const_spec(w_out.shape),
        ],
        out_specs=pl.BlockSpec((tm, d), lambda i: (i, 0)),
        compiler_params=_params(("parallel",)),
        name="branch_mix",
    )(o, hc, gates, gates, x2, w_att, w_conv, w_out)


def _norm_t_kernel(h_ref, g_ref, o_ref):
    o_ref[...] = _rms(h_ref[...], g_ref[...]).T.astype(BF16)


def _norm_transpose(h, g, tt):
    t, d = h.shape
    return pl.pallas_call(
        _norm_t_kernel,
        out_shape=jax.ShapeDtypeStruct((d, t), BF16),
        grid=(t // tt,),
        in_specs=[pl.BlockSpec((tt, d), lambda i: (i, 0)), pl.BlockSpec((1, d), lambda i: (0, 0))],
        out_specs=pl.BlockSpec((d, tt), lambda i: (0, i)),
        compiler_params=_params(("parallel",)),
        name="peer_norm_t",
    )(h, g)


def _scores_kernel(xt_ref, wq_ref, keys_ref, s_ref):
    qt = jnp.dot(wq_ref[...], xt_ref[...], preferred_element_type=F32).astype(BF16)
    for hp in range(2 * PEER_HEADS):
        rows = slice(hp * PEER_N_KEYS, (hp + 1) * PEER_N_KEYS)
        s_ref[rows, :] = jnp.dot(keys_ref[hp], qt[rows, :], preferred_element_type=F32)


def _peer_scores(xt, wq_t, keys, tt):
    d, t = xt.shape
    return pl.pallas_call(
        _scores_kernel,
        out_shape=jax.ShapeDtypeStruct((d, t), F32),
        grid=(t // tt,),
        in_specs=[pl.BlockSpec((d, tt), lambda i: (0, i)), _const_spec(wq_t.shape), _const_spec(keys.shape)],
        out_specs=pl.BlockSpec((d, tt), lambda i: (0, i)),
        compiler_params=_params(("parallel",)),
        name="peer_scores",
    )(xt, wq_t, keys)


TOPN = PEER_TOPK + 1


def _cand_pairs():
    return [(k, l) for k in range(TOPN) for l in range(TOPN) if (k + 1) * (l + 1) <= TOPN]


CAND_PAIRS = _cand_pairs()
CAND_ROWS = -(-len(CAND_PAIRS) // SUBLANES) * SUBLANES


def _top_values(x, n):
    out = []
    for _ in range(n):
        m = jnp.max(x, axis=0, keepdims=True)
        out.append(m)
        x = jnp.where(x == m, -jnp.inf, x)
    return out


def _topk_kernel(s_ref, c_ref, d_ref, e1_ref, cand_ref):
    for h in range(PEER_HEADS):
        s0 = s_ref[h, 0]
        s1 = s_ref[h, 1]
        a = _top_values(s0, TOPN)
        b = _top_values(s1, TOPN)
        cand_ref[...] = jnp.full(cand_ref.shape, -jnp.inf, F32)
        for r, (k, l) in enumerate(CAND_PAIRS):
            cand_ref[r:r + 1, :] = a[k] + b[l]
        cand = cand_ref[...]
        tops = _top_values(cand, TOPN)
        thr = 0.5 * (tops[PEER_TOPK - 1] + tops[PEER_TOPK])
        mx = a[0] + b[0]
        z = jnp.sum(jnp.where(cand >= thr, jnp.exp(cand - mx), 0.0), axis=0, keepdims=True)
        c_ref[h] = thr - s0
        d_ref[h] = jnp.exp(s0 - a[0]) / z
        e1_ref[h] = jnp.exp(s1 - b[0])


def _peer_topk(scores4, tt):
    hh, _, nk, t = scores4.shape
    out = jax.ShapeDtypeStruct((hh, nk, t), F32)
    spec = pl.BlockSpec((hh, nk, tt), lambda i: (0, 0, i))
    return pl.pallas_call(
        _topk_kernel,
        out_shape=(out, out, out),
        grid=(t // tt,),
        in_specs=[pl.BlockSpec((hh, 2, nk, tt), lambda i: (0, 0, 0, i))],
        out_specs=(spec, spec, spec),
        scratch_shapes=[pltpu.VMEM((CAND_ROWS, tt), F32)],
        compiler_params=_params(("parallel",)),
        name="peer_topk",
    )(scores4)


PEER_ROWS = 16


def _peer_dense_kernel(xt_ref, u_ref, vt_ref, c_ref, d_ref, s1_ref, e1_ref, yt_ref, at_ref, wt_ref,
                       *, tt, ni):
    e = pl.program_id(1)

    @pl.when(e == 0)
    def _():
        yt_ref[...] = jnp.zeros(yt_ref.shape, F32)

    at_ref[...] = jnp.dot(u_ref[...], xt_ref[...], preferred_element_type=F32)

    def lane_body(lg, carry):
        lanes = pl.ds(pl.multiple_of(lg * LANES, LANES), LANES)
        for r in range(ni):
            cb = [jnp.broadcast_to(c_ref[h, r:r + 1, lanes], (PEER_ROWS, LANES)) for h in range(PEER_HEADS)]
            db = [jnp.broadcast_to(d_ref[h, r:r + 1, lanes], (PEER_ROWS, LANES)) for h in range(PEER_HEADS)]
            for jp in range(PEER_N_KEYS // PEER_ROWS):
                rows = slice(jp * PEER_ROWS, (jp + 1) * PEER_ROWS)
                g = jnp.zeros((PEER_ROWS, LANES), F32)
                for h in range(PEER_HEADS):
                    g = g + jnp.where(s1_ref[h, rows, lanes] >= cb[h], e1_ref[h, rows, lanes] * db[h], 0.0)
                arow = slice(r * PEER_N_KEYS + jp * PEER_ROWS, r * PEER_N_KEYS + (jp + 1) * PEER_ROWS)
                a = at_ref[arow, lanes]
                w = 0.5 * a * (1.0 + lax.erf(a * (2.0 ** -0.5))) * g
                wt_ref[arow, lanes] = w.astype(BF16)
        return carry

    lax.fori_loop(0, tt // LANES, lane_body, 0)
    yt_ref[...] += jnp.dot(vt_ref[...], wt_ref[...], preferred_element_type=F32)


def _peer_dense(xt, u, vt, c, d, scores4, e1, tt, ni):
    dm, t = xt.shape
    n_exp = u.shape[0]
    et = ni * PEER_N_KEYS
    hh, nk, _ = c.shape
    kern = functools.partial(_peer_dense_kernel, tt=tt, ni=ni)
    tok_spec = pl.BlockSpec((hh, nk, tt), lambda i, e: (0, 0, i), pipeline_mode=pl.Buffered(1))
    return pl.pallas_call(
        kern,
        out_shape=jax.ShapeDtypeStruct((dm, t), F32),
        grid=(t // tt, n_exp // et),
        in_specs=[
            pl.BlockSpec((dm, tt), lambda i, e: (0, i)),
            pl.BlockSpec((et, dm), lambda i, e: (e, 0)),
            pl.BlockSpec((dm, et), lambda i, e: (0, e)),
            pl.BlockSpec((hh, ni, tt), lambda i, e: (0, e, i)),
            pl.BlockSpec((hh, ni, tt), lambda i, e: (0, e, i)),
            pl.BlockSpec((hh, None, nk, tt), lambda i, e: (0, 1, 0, i), pipeline_mode=pl.Buffered(1)),
            tok_spec,
        ],
        out_specs=pl.BlockSpec((dm, tt), lambda i, e: (0, i)),
        scratch_shapes=[pltpu.VMEM((et, tt), F32), pltpu.VMEM((et, tt), BF16)],
        compiler_params=_params(("parallel", "arbitrary")),
        name="peer_dense",
    )(xt, u, vt, c, d, scores4, e1)


def _final_kernel(h_ref, yt_ref, g_ref, o_ref):
    o_ref[...] = _rms(h_ref[...] + yt_ref[...].T, g_ref[...])


def _final(h, yt, g, tt):
    t, d = h.shape
    return pl.pallas_call(
        _final_kernel,
        out_shape=jax.ShapeDtypeStruct((t, d), F32),
        grid=(t // tt,),
        in_specs=[pl.BlockSpec((tt, d), lambda i: (i, 0)), pl.BlockSpec((d, tt), lambda i: (0, i)),
                  pl.BlockSpec((1, d), lambda i: (0, 0))],
        out_specs=pl.BlockSpec((tt, d), lambda i: (i, 0)),
        compiler_params=_params(("parallel",)),
        name="final_norm",
    )(h, yt, g)


def kernel(x, norm_mix_g, w_in, b_gate, lam_q1, lam_k1, lam_q2, lam_k2, subln_g, w_att_out, conv_w,
           conv_b, conv_ln_g, conv_ln_b, w_conv_out, w_out, rel_bias, norm_ffn_g, peer_w_q,
           peer_sub_keys, peer_u, peer_v, final_norm_g):
    batch, seq, d = x.shape
    t = batch * seq
    depth = w_in.shape[0]
    assert depth == 1 and d == D_MODEL
    h = x.reshape(t, d)

    c_q, c_k, c_v = ATT_WIDTH, 2 * ATT_WIDTH, 3 * ATT_WIDTH
    c_glu = c_v + 2 * CONV_CHANNELS
    tq = 512
    bias_diag, bias_sub = _bias_tiles(rel_bias, tq, tq)

    l = 0
    w = w_in[l]
    w_qkv = jnp.concatenate([w[:, :c_q] * (HEAD_DIM ** -0.5), w[:, c_q:c_v]], axis=1).astype(BF16)
    w_glu_a = w[:, c_v:c_v + CONV_CHANNELS].astype(BF16)
    w_glu_b = w[:, c_v + CONV_CHANNELS:c_glu].astype(BF16)
    w_gate = w[:, c_glu:].astype(BF16)
    g_mix = norm_mix_g[l].reshape(1, d)

    qkv = _norm_proj("proj_qkv", _proj_plain_kernel, h, g_mix, [w_qkv], [], 3 * ATT_WIDTH, BF16, 1024, 512)
    hglu = _norm_proj("proj_glu", _proj_glu_kernel, h, g_mix, [w_glu_a, w_glu_b], [], CONV_CHANNELS, F32,
                      1024, 512)
    gates = _norm_proj("proj_gates", _proj_gate_kernel, h, g_mix, [w_gate], [b_gate[l].reshape(1, -1)],
                       2 * d, F32, 1024, 512)

    lam = (jnp.exp(jnp.sum(lam_q1[l].astype(F32) * lam_k1[l].astype(F32)))
           - jnp.exp(jnp.sum(lam_q2[l].astype(F32) * lam_k2[l].astype(F32))) + LAM_INIT).reshape(1)
    o = _diff_attention(qkv, lam, bias_diag, bias_sub, subln_g[l].reshape(HEAD_W, 1), batch, seq, tq)

    hc = _conv_module(hglu, conv_w[l], conv_b[l].reshape(1, -1), conv_ln_g[l].reshape(1, -1),
                      conv_ln_b[l].reshape(1, -1), seq, 256)

    h1 = _mix(o, hc, gates, h, w_att_out[l].astype(BF16), w_conv_out[l].astype(BF16),
              w_out[l].astype(BF16), 256)

    xt = _norm_transpose(h1, norm_ffn_g[l].reshape(1, d), 512)
    keys = peer_sub_keys[l].reshape(2 * PEER_HEADS, PEER_N_KEYS, -1).astype(BF16)
    scores = _peer_scores(xt, peer_w_q[l].T.astype(BF16), keys, 512)
    scores4 = scores.reshape(PEER_HEADS, 2, PEER_N_KEYS, t)
    c, dd, e1 = _peer_topk(scores4, 512)
    yt = _peer_dense(xt, peer_u[l].astype(BF16), peer_v[l].T.astype(BF16), c, dd, scores4, e1, 512, 8)
    out = _final(h1, yt, final_norm_g.reshape(1, d), 512)
    return out.reshape(batch, seq, d)
```

```python
import functools
import math

import jax
import jax.numpy as jnp
from jax import lax
from jax.experimental import pallas as pl
from jax.experimental.pallas import tpu as pltpu

F32 = jnp.float32
BF16 = jnp.bfloat16

D_MODEL = 2048
N_HEADS = 8
HEAD_DIM = 64
HEAD_W = 2 * HEAD_DIM
ATT_WIDTH = N_HEADS * HEAD_W
CONV_CHANNELS = 1024
CONV_TAPS = 31
CHUNK = 64
N_BUCKETS = 32
MAX_DISTANCE = 128
PEER_HEADS = 8
PEER_N_KEYS = 128
PEER_TOPK = 16
EPS = 1e-6
NEG = -1e30
LAM_INIT = 0.8 - 0.6 * math.exp(-0.3 * 0)

LANES = 128
SUBLANES = 8
MXU_DIM = 256
VMEM_LIMIT = 56 * 1024 * 1024


def _params(sem, flags=None):
    return pltpu.CompilerParams(dimension_semantics=sem, vmem_limit_bytes=VMEM_LIMIT, flags=flags)


def _rms(x, g):
    ms = jnp.mean(x * x, axis=-1, keepdims=True)
    return x * lax.rsqrt(ms + EPS) * g


def _norm_once(x_ref, g_ref, n_ref):
    @pl.when(pl.program_id(1) == 0)
    def _():
        n_ref[...] = _rms(x_ref[...], g_ref[...]).astype(BF16)


def _proj_plain_kernel(x_ref, g_ref, w_ref, o_ref, n_ref):
    _norm_once(x_ref, g_ref, n_ref)
    o_ref[...] = jnp.dot(n_ref[...], w_ref[...], preferred_element_type=F32).astype(o_ref.dtype)


def _proj_glu_kernel(x_ref, g_ref, wa_ref, wb_ref, o_ref, n_ref):
    _norm_once(x_ref, g_ref, n_ref)
    n = n_ref[...]
    a = jnp.dot(n, wa_ref[...], preferred_element_type=F32)
    b = jnp.dot(n, wb_ref[...], preferred_element_type=F32)
    o_ref[...] = a * jax.nn.sigmoid(b)


def _proj_gate_kernel(x_ref, g_ref, w_ref, b_ref, o_ref, n_ref):
    _norm_once(x_ref, g_ref, n_ref)
    z = jnp.dot(n_ref[...], w_ref[...], preferred_element_type=F32) + b_ref[...]
    o_ref[...] = jax.nn.sigmoid(z)


def _norm_proj(name, body, x2, g, ws, extras, n_out, out_dtype, tm, tn):
    t, d = x2.shape
    grid = (t // tm, n_out // tn)
    in_specs = [pl.BlockSpec((tm, d), lambda i, j: (i, 0)),
                pl.BlockSpec((1, d), lambda i, j: (0, 0))]
    in_specs += [pl.BlockSpec((d, tn), lambda i, j: (0, j)) for _ in ws]
    in_specs += [pl.BlockSpec((1, tn), lambda i, j: (0, j)) for _ in extras]
    return pl.pallas_call(
        body,
        out_shape=jax.ShapeDtypeStruct((t, n_out), out_dtype),
        grid=grid,
        in_specs=in_specs,
        out_specs=pl.BlockSpec((tm, tn), lambda i, j: (i, j)),
        scratch_shapes=[pltpu.VMEM((tm, d), BF16)],
        compiler_params=_params(("parallel", "arbitrary")),
        name=name,
    )(x2, g, *ws, *extras)


def _rel_bucket(rel):
    nb = N_BUCKETS // 2
    max_exact = nb // 2
    ret = (rel > 0).astype(jnp.int32) * nb
    n = jnp.abs(rel)
    nf = jnp.maximum(n, 1).astype(F32)
    large = max_exact + (jnp.log(nf / max_exact) / math.log(MAX_DISTANCE / max_exact)
                         * (nb - max_exact)).astype(jnp.int32)
    large = jnp.minimum(large, nb - 1)
    return ret + jnp.where(n < max_exact, n, large)


def _bias_tiles(rel_bias, tq, tk):
    far = rel_bias[N_BUCKETS // 2 - 1]
    span = tq + tk - 1

    def expand(key_offset):
        rel = jnp.arange(span, dtype=jnp.int32) - (tq - 1) + key_offset
        table = (rel_bias[_rel_bucket(rel)] - far).T.astype(F32)
        rolled = jnp.roll(table, -(tq - 1), axis=1)
        flat = jnp.tile(rolled, (1, tq))[:, :tq * (span - 1)]
        tile_qk = flat.reshape(-1, tq, span - 1)[:, :, :tk]
        return jnp.transpose(tile_qk, (0, 2, 1))

    return expand(0), expand(-tk)


def _attn_kernel(lam_ref, q_ref, k_ref, v_ref, bd_ref, bs_ref, g_ref, o_ref, vt_ref, m_ref, l_ref, acc_ref,
                 *, tq, tk):
    qi = pl.program_id(1)

    @pl.when(qi == 0)
    def _():
        for j in range(vt_ref.shape[0]):
            vt_ref[j] = v_ref[j * tk:(j + 1) * tk, :].T

    q = q_ref[...]
    lane = lax.broadcasted_iota(jnp.int32, q.shape, 1)
    zero = jnp.zeros_like(q)
    qm = (jnp.where(lane < HEAD_DIM, q, zero), jnp.where(lane >= HEAD_DIM, q, zero))

    m_ref[...] = jnp.full(m_ref.shape, NEG, F32)
    l_ref[...] = jnp.zeros(l_ref.shape, F32)
    acc_ref[...] = jnp.zeros(acc_ref.shape, F32)

    def update(j, fix):
        kt = k_ref[pl.ds(pl.multiple_of(j * tk, tk), tk), :]
        vt = vt_ref[j]
        for mp in range(2):
            s = lax.dot_general(kt, qm[mp], (((1,), (1,)), ((), ())), preferred_element_type=F32)
            s = fix(s)
            m_old = m_ref[mp]
            m_new = jnp.maximum(m_old, jnp.max(s, axis=0, keepdims=True))
            alpha = jnp.exp(m_old - m_new)
            p = jnp.exp(s - m_new)
            l_ref[mp] = alpha * l_ref[mp] + jnp.sum(p, axis=0, keepdims=True)
            acc_ref[mp] = alpha * acc_ref[mp] + jnp.dot(vt, p.astype(BF16), preferred_element_type=F32)
            m_ref[mp] = m_new

    def far_body(j, carry):
        update(j, lambda s: s)
        return carry

    lax.fori_loop(0, qi - 1, far_body, 0)

    @pl.when(qi >= 1)
    def _():
        update(qi - 1, lambda s: s + bs_ref[0])

    kc = lax.broadcasted_iota(jnp.int32, (tk, tq), 0) // CHUNK
    qc = lax.broadcasted_iota(jnp.int32, (tk, tq), 1) // CHUNK
    visible = kc <= qc
    update(qi, lambda s: jnp.where(visible, s + bd_ref[0], NEG))

    lam = lam_ref[0]
    ot = acc_ref[0] / l_ref[0] - lam * (acc_ref[1] / l_ref[1])
    ms = jnp.mean(ot * ot, axis=0, keepdims=True)
    ot = ot * lax.rsqrt(ms + EPS) * g_ref[...] * (1.0 - LAM_INIT)
    o_ref[...] = ot.T.astype(o_ref.dtype)


def _diff_attention(qkv, lam, bias_diag, bias_sub, subln_g, batch, seq, tq):
    tk = tq
    nq = seq // tq
    kern = functools.partial(_attn_kernel, tq=tq, tk=tk)
    return pl.pallas_call(
        kern,
        out_shape=jax.ShapeDtypeStruct((batch * seq, ATT_WIDTH), BF16),
        grid=(batch * N_HEADS, nq),
        in_specs=[
            pl.BlockSpec(memory_space=pltpu.SMEM),
            pl.BlockSpec((tq, HEAD_W), lambda bh, i: ((bh // N_HEADS) * nq + i, bh % N_HEADS)),
            pl.BlockSpec((seq, HEAD_W), lambda bh, i: (bh // N_HEADS, N_HEADS + bh % N_HEADS)),
            pl.BlockSpec((seq, HEAD_W), lambda bh, i: (bh // N_HEADS, 2 * N_HEADS + bh % N_HEADS)),
            pl.BlockSpec((1, tk, tq), lambda bh, i: (bh % N_HEADS, 0, 0)),
            pl.BlockSpec((1, tk, tq), lambda bh, i: (bh % N_HEADS, 0, 0)),
            pl.BlockSpec((HEAD_W, 1), lambda bh, i: (0, 0)),
        ],
        out_specs=pl.BlockSpec((tq, HEAD_W), lambda bh, i: ((bh // N_HEADS) * nq + i, bh % N_HEADS)),
        scratch_shapes=[pltpu.VMEM((seq // tk, HEAD_W, tk), BF16),
                        pltpu.VMEM((2, 1, tq), F32), pltpu.VMEM((2, 1, tq), F32),
                        pltpu.VMEM((2, HEAD_W, tq), F32)],
        compiler_params=_params(("parallel", "arbitrary")),
        name="diff_attention",
    )(lam, qkv, qkv, qkv, bias_diag, bias_sub, subln_g)


CONV_HALO = 32
CONV_ROWS = 32


def _conv_kernel(cur_ref, halo_ref, w_ref, b_ref, lg_ref, lb_ref, o_ref, ext_ref, y_ref,
                 *, ts, tiles_per_seq):
    i = pl.program_id(0)
    first = (i % tiles_per_seq) == 0
    halo = halo_ref[...]
    ext_ref[0:CONV_HALO, :] = jnp.where(first, jnp.zeros_like(halo), halo)
    ext_ref[CONV_HALO:, :] = cur_ref[...]
    lead = CONV_HALO - (CONV_TAPS - 1)
    nsub = CONV_ROWS // SUBLANES

    for c in range(CONV_CHANNELS // LANES):
        lanes = slice(c * LANES, (c + 1) * LANES)
        wb = [jnp.broadcast_to(w_ref[j:j + 1, lanes], (SUBLANES, LANES)) for j in range(CONV_TAPS)]
        bb = jnp.broadcast_to(b_ref[:, lanes], (SUBLANES, LANES))

        def row_body(r, carry, lanes=lanes, wb=wb, bb=bb):
            r0 = pl.multiple_of(r * CONV_ROWS, CONV_ROWS)
            win_rows = CONV_ROWS + CONV_HALO
            win = ext_ref[pl.ds(r0, win_rows), lanes]
            acc = [bb] * nsub
            for sh in range(SUBLANES):
                rolled = win if sh == 0 else pltpu.roll(win, win_rows - sh, axis=0)
                for a in range(win_rows // SUBLANES):
                    j = a * SUBLANES + sh - lead
                    if 0 <= j < CONV_TAPS:
                        for u in range(nsub):
                            lo = (a + u) * SUBLANES
                            acc[u] = acc[u] + rolled[lo:lo + SUBLANES] * wb[j]
            for u in range(nsub):
                y_ref[pl.ds(r0 + u * SUBLANES, SUBLANES), lanes] = acc[u]
            return carry

        lax.fori_loop(0, ts // CONV_ROWS, row_body, 0)

    y = y_ref[...]
    mu = jnp.mean(y, axis=-1, keepdims=True)
    yc = y - mu
    var = jnp.mean(yc * yc, axis=-1, keepdims=True)
    z = yc * lax.rsqrt(var + EPS) * lg_ref[...] + lb_ref[...]
    o_ref[...] = (z * jax.nn.sigmoid(z)).astype(o_ref.dtype)


def _conv_module(hglu, conv_w, conv_b, ln_g, ln_b, seq, ts):
    t, c = hglu.shape
    kern = functools.partial(_conv_kernel, ts=ts, tiles_per_seq=seq // ts)
    hb = ts // CONV_HALO
    return pl.pallas_call(
        kern,
        out_shape=jax.ShapeDtypeStruct((t, c), BF16),
        grid=(t // ts,),
        in_specs=[
            pl.BlockSpec((ts, c), lambda i: (i, 0)),
            pl.BlockSpec((CONV_HALO, c), lambda i: (jnp.maximum(i * hb - 1, 0), 0)),
            pl.BlockSpec((CONV_TAPS, c), lambda i: (0, 0)),
            pl.BlockSpec((1, c), lambda i: (0, 0)),
            pl.BlockSpec((1, c), lambda i: (0, 0)),
            pl.BlockSpec((1, c), lambda i: (0, 0)),
        ],
        out_specs=pl.BlockSpec((ts, c), lambda i: (i, 0)),
        scratch_shapes=[pltpu.VMEM((ts + CONV_HALO, c), F32), pltpu.VMEM((ts, c), F32)],
        compiler_params=_params(("parallel",)),
        name="conv_module",
    )(hglu, hglu, conv_w, conv_b, ln_g, ln_b)


def _mix_kernel(o_ref, hc_ref, ga_ref, gc_ref, x_ref, wa_ref, wc_ref, wo_ref, h_ref):
    ya = jnp.dot(o_ref[...], wa_ref[...], preferred_element_type=F32)
    yc = jnp.dot(hc_ref[...], wc_ref[...], preferred_element_type=F32)
    mix = (ga_ref[...] * ya + gc_ref[...] * yc).astype(BF16)
    h_ref[...] = x_ref[...] + jnp.dot(mix, wo_ref[...], preferred_element_type=F32)


def _const_spec(shape):
    nd = len(shape)
    return pl.BlockSpec(shape, lambda *_: (0,) * nd, pipeline_mode=pl.Buffered(1))


def _mix(o, hc, gates, x2, w_att, w_conv, w_out, tm):
    t, d = x2.shape
    return pl.pallas_call(
        _mix_kernel,
        out_shape=jax.ShapeDtypeStruct((t, d), F32),
        grid=(t // tm,),
        in_specs=[
            pl.BlockSpec((tm, ATT_WIDTH), lambda i: (i, 0)),
            pl.BlockSpec((tm, CONV_CHANNELS), lambda i: (i, 0)),
            pl.BlockSpec((tm, d), lambda i: (i, 0)),
            pl.BlockSpec((tm, d), lambda i: (i, 1)),
            pl.BlockSpec((tm, d), lambda i: (i, 0)),
            _const_spec(w_att.shape), _const_spec(w_conv.shape), _const_spec(w_out.shape),
        ],
        out_specs=pl.BlockSpec((tm, d), lambda i: (i, 0)),
        compiler_params=_params(("parallel",)),
        name="branch_mix",
    )(o, hc, gates, gates, x2, w_att, w_conv, w_out)


def _norm_t_kernel(h_ref, g_ref, o_ref):
    o_ref[...] = _rms(h_ref[...], g_ref[...]).T.astype(BF16)


def _norm_transpose(h, g, tt):
    t, d = h.shape
    return pl.pallas_call(
        _norm_t_kernel,
        out_shape=jax.ShapeDtypeStruct((d, t), BF16),
        grid=(t // tt,),
        in_specs=[pl.BlockSpec((tt, d), lambda i: (i, 0)), pl.BlockSpec((1, d), lambda i: (0, 0))],
        out_specs=pl.BlockSpec((d, tt), lambda i: (0, i)),
        compiler_params=_params(("parallel",)),
        name="peer_norm_t",
    )(h, g)


def _scores_kernel(xt_ref, wq_ref, keys_ref, s_ref):
    qt = jnp.dot(wq_ref[...], xt_ref[...], preferred_element_type=F32).astype(BF16)
    for hp in range(2 * PEER_HEADS):
        rows = slice(hp * PEER_N_KEYS, (hp + 1) * PEER_N_KEYS)
        s_ref[rows, :] = jnp.dot(keys_ref[hp], qt[rows, :], preferred_element_type=F32)


def _peer_scores(xt, wq_t, keys, tt):
    d, t = xt.shape
    return pl.pallas_call(
        _scores_kernel,
        out_shape=jax.ShapeDtypeStruct((d, t), F32),
        grid=(t // tt,),
        in_specs=[pl.BlockSpec((d, tt), lambda i: (0, i)), _const_spec(wq_t.shape), _const_spec(keys.shape)],
        out_specs=pl.BlockSpec((d, tt), lambda i: (0, i)),
        compiler_params=_params(("parallel",)),
        name="peer_scores",
    )(xt, wq_t, keys)


TOPN = PEER_TOPK + 1


def _cand_pairs():
    return [(k, l) for k in range(TOPN) for l in range(TOPN) if (k + 1) * (l + 1) <= TOPN]


CAND_PAIRS = _cand_pairs()
CAND_ROWS = -(-len(CAND_PAIRS) // SUBLANES) * SUBLANES


def _top_values(x, n):
    out = []
    for _ in range(n):
        m = jnp.max(x, axis=0, keepdims=True)
        out.append(m)
        x = jnp.where(x == m, -jnp.inf, x)
    return out


def _topk_kernel(s_ref, c_ref, d_ref, s1_ref, e1_ref, cand_ref):
    tt = s_ref.shape[-1]

    def put(ref, h, val):
        for lg in range(tt // LANES):
            ref[lg, h] = val[:, lg * LANES:(lg + 1) * LANES]

    for h in range(PEER_HEADS):
        s0 = s_ref[h, 0]
        s1 = s_ref[h, 1]
        a = _top_values(s0, TOPN)
        b = _top_values(s1, TOPN)
        cand_ref[...] = jnp.full(cand_ref.shape, -jnp.inf, F32)
        for r, (k, l) in enumerate(CAND_PAIRS):
            cand_ref[r:r + 1, :] = a[k] + b[l]
        cand = cand_ref[...]
        tops = _top_values(cand, TOPN)
        thr = 0.5 * (tops[PEER_TOPK - 1] + tops[PEER_TOPK])
        mx = a[0] + b[0]
        z = jnp.sum(jnp.where(cand >= thr, jnp.exp(cand - mx), 0.0), axis=0, keepdims=True)
        put(c_ref, h, thr - s0)
        put(d_ref, h, jnp.exp(s0 - a[0]) / z)
        put(s1_ref, h, s1)
        put(e1_ref, h, jnp.exp(s1 - b[0]))


def _peer_topk(scores4, tt):
    hh, _, nk, t = scores4.shape
    out = jax.ShapeDtypeStruct((t // LANES, hh, nk, LANES), F32)
    spec = pl.BlockSpec((tt // LANES, hh, nk, LANES), lambda i: (i, 0, 0, 0))
    return pl.pallas_call(
        _topk_kernel,
        out_shape=(out, out, out, out),
        grid=(t // tt,),
        in_specs=[pl.BlockSpec((hh, 2, nk, tt), lambda i: (0, 0, 0, i))],
        out_specs=(spec, spec, spec, spec),
        scratch_shapes=[pltpu.VMEM((CAND_ROWS, tt), F32)],
        compiler_params=_params(("parallel",)),
        name="peer_topk",
    )(scores4)


PEER_ROWS = 16


def _peer_dense_kernel(xt_ref, u_ref, vt_ref, c_ref, d_ref, s1_ref, e1_ref, yt_ref, at0, at1, wt0, wt1, yacc,
                       *, tt, ni, ne):
    s = pl.program_id(0)

    @pl.when(s == 0)
    def _():
        for ref in (at0, at1, wt0, wt1):
            ref[...] = jnp.zeros(ref.shape, ref.dtype)

    @pl.when((s == 0) | ((s >= 2) & ((s - 2) % ne == 0)))
    def _():
        yacc[...] = jnp.zeros(yacc.shape, F32)

    def weights(r, lg, a_cur, w_new):
        lanes = slice(lg * LANES, (lg + 1) * LANES)
        cb = [jnp.broadcast_to(c_ref[lg, h, r:r + 1, :], (PEER_ROWS, LANES)) for h in range(PEER_HEADS)]
        db = [jnp.broadcast_to(d_ref[lg, h, r:r + 1, :], (PEER_ROWS, LANES)) for h in range(PEER_HEADS)]
        for jp in range(PEER_N_KEYS // PEER_ROWS):
            rows = slice(jp * PEER_ROWS, (jp + 1) * PEER_ROWS)
            g = jnp.zeros((PEER_ROWS, LANES), F32)
            for h in range(PEER_HEADS):
                g = g + jnp.where(s1_ref[lg, h, rows, :] >= cb[h], e1_ref[lg, h, rows, :] * db[h], 0.0)
            arow = slice(r * PEER_N_KEYS + jp * PEER_ROWS, r * PEER_N_KEYS + (jp + 1) * PEER_ROWS)
            a = a_cur[arow, lanes]
            w = 0.5 * a * (1.0 + lax.erf(a * (2.0 ** -0.5))) * g
            w_new[arow, lanes] = w.astype(BF16)

    def step(a_new, a_cur, w_new, w_cur):
        md = MXU_DIM
        d_model = yacc.shape[0]
        et = ni * PEER_N_KEYS
        kh = d_model // 2
        mm = []
        for m in range(d_model // md):
            for n in range(tt // md):
                mm.append(("y", m, n, 0))
        for m in range(et // md):
            for n in range(tt // md):
                mm.append(("a", m, n, 0))
                mm.append(("a", m, n, 1))
        vp = [(r, lg) for r in range(ni) for lg in range(tt // LANES)]
        assert len(mm) == len(vp)
        half = len(mm) // 2
        order = [x for pair in zip(mm[:half], mm[half:]) for x in pair]
        for (kind, m, n, k), (r, lg) in zip(order, vp):
            ms = slice(m * md, (m + 1) * md)
            ns = slice(n * md, (n + 1) * md)
            if kind == "y":
                yacc[ms, ns] += jnp.dot(vt_ref[ms, :], w_cur[:, ns], preferred_element_type=F32)
            else:
                ks = slice(k * kh, (k + 1) * kh)
                part = jnp.dot(u_ref[ms, ks], xt_ref[ks, ns], preferred_element_type=F32)
                if k == 0:
                    a_new[ms, ns] = part
                else:
                    a_new[ms, ns] += part
            weights(r, lg, a_cur, w_new)

    @pl.when(s % 2 == 0)
    def _():
        step(at0, at1, wt0, wt1)

    @pl.when(s % 2 == 1)
    def _():
        step(at1, at0, wt1, wt0)

    @pl.when((s >= 2) & ((s - 2) % ne == ne - 1))
    def _():
        yt_ref[...] = yacc[...]


def _peer_dense(xt, u, vt, c, d, s1, e1, tt, ni):
    dm, t = xt.shape
    n_exp = u.shape[0]
    et = ni * PEER_N_KEYS
    ne = n_exp // et
    n_tiles = (t // tt) * ne
    hh, nk = c.shape[1], c.shape[2]
    lgs = tt // LANES
    kern = functools.partial(_peer_dense_kernel, tt=tt, ni=ni, ne=ne)

    def stage(lag):
        return lambda s: jnp.clip(s - lag, 0, n_tiles - 1)

    mm1, vpu, mm2 = stage(0), stage(1), stage(2)
    return pl.pallas_call(
        kern,
        out_shape=jax.ShapeDtypeStruct((dm, t), F32),
        grid=(n_tiles + 2,),
        in_specs=[
            pl.BlockSpec((dm, tt), lambda s: (0, mm1(s) // ne)),
            pl.BlockSpec((et, dm), lambda s: (mm1(s) % ne, 0)),
            pl.BlockSpec((dm, et), lambda s: (0, mm2(s) % ne)),
            pl.BlockSpec((lgs, hh, ni, LANES), lambda s: (vpu(s) // ne, 0, vpu(s) % ne, 0)),
            pl.BlockSpec((lgs, hh, ni, LANES), lambda s: (vpu(s) // ne, 0, vpu(s) % ne, 0)),
            pl.BlockSpec((lgs, hh, nk, LANES), lambda s: (vpu(s) // ne, 0, 0, 0)),
            pl.BlockSpec((lgs, hh, nk, LANES), lambda s: (vpu(s) // ne, 0, 0, 0)),
        ],
        out_specs=pl.BlockSpec((dm, tt), lambda s: (0, mm2(s) // ne)),
        scratch_shapes=[pltpu.VMEM((et, tt), F32), pltpu.VMEM((et, tt), F32),
                        pltpu.VMEM((et, tt), BF16), pltpu.VMEM((et, tt), BF16),
                        pltpu.VMEM((dm, tt), F32)],
        compiler_params=_params(("arbitrary",)),
        name="peer_dense",
    )(xt, u, vt, c, d, s1, e1)


def _peer_block(h1, g_ffn, w_q, sub_keys, u_tab, v_tab, g_final, tt, ni=8):
    t, d = h1.shape
    xt = _norm_transpose(h1, g_ffn.reshape(1, d), tt)
    keys = sub_keys.reshape(2 * PEER_HEADS, PEER_N_KEYS, -1).astype(BF16)
    scores = _peer_scores(xt, w_q.T.astype(BF16), keys, tt)
    scores4 = scores.reshape(PEER_HEADS, 2, PEER_N_KEYS, t)
    c, dd, s1, e1 = _peer_topk(scores4, tt)
    yt = _peer_dense(xt, u_tab.astype(BF16), v_tab.T.astype(BF16), c, dd, s1, e1, tt, ni)
    return _final(h1, yt, g_final.reshape(1, d), tt)


def _final_kernel(h_ref, yt_ref, g_ref, o_ref):
    o_ref[...] = _rms(h_ref[...] + yt_ref[...].T, g_ref[...])


def _final(h, yt, g, tt):
    t, d = h.shape
    return pl.pallas_call(
        _final_kernel,
        out_shape=jax.ShapeDtypeStruct((t, d), F32),
        grid=(t // tt,),
        in_specs=[pl.BlockSpec((tt, d), lambda i: (i, 0)), pl.BlockSpec((d, tt), lambda i: (0, i)),
                  pl.BlockSpec((1, d), lambda i: (0, 0))],
        out_specs=pl.BlockSpec((tt, d), lambda i: (i, 0)),
        compiler_params=_params(("parallel",)),
        name="final_norm",
    )(h, yt, g)


def kernel(x, norm_mix_g, w_in, b_gate, lam_q1, lam_k1, lam_q2, lam_k2, subln_g, w_att_out, conv_w,
           conv_b, conv_ln_g, conv_ln_b, w_conv_out, w_out, rel_bias, norm_ffn_g, peer_w_q,
           peer_sub_keys, peer_u, peer_v, final_norm_g):
    batch, seq, d = x.shape
    t = batch * seq
    depth = w_in.shape[0]
    assert depth == 1 and d == D_MODEL
    h = x.reshape(t, d)

    c_q, c_k, c_v = ATT_WIDTH, 2 * ATT_WIDTH, 3 * ATT_WIDTH
    c_glu = c_v + 2 * CONV_CHANNELS
    tq = 512
    bias_diag, bias_sub = _bias_tiles(rel_bias, tq, tq)

    l = 0
    w = w_in[l]
    w_qkv = jnp.concatenate([w[:, :c_q] * (HEAD_DIM ** -0.5), w[:, c_q:c_v]], axis=1).astype(BF16)
    w_glu_a = w[:, c_v:c_v + CONV_CHANNELS].astype(BF16)
    w_glu_b = w[:, c_v + CONV_CHANNELS:c_glu].astype(BF16)
    w_gate = w[:, c_glu:].astype(BF16)
    g_mix = norm_mix_g[l].reshape(1, d)

    qkv = _norm_proj("proj_qkv", _proj_plain_kernel, h, g_mix, [w_qkv], [], 3 * ATT_WIDTH, BF16, 1024, 512)
    hglu = _norm_proj("proj_glu", _proj_glu_kernel, h, g_mix, [w_glu_a, w_glu_b], [], CONV_CHANNELS, F32,
                      1024, 512)
    gates = _norm_proj("proj_gates", _proj_gate_kernel, h, g_mix, [w_gate], [b_gate[l].reshape(1, -1)],
                       2 * d, F32, 1024, 512)

    lam = (jnp.exp(jnp.sum(lam_q1[l].astype(F32) * lam_k1[l].astype(F32)))
           - jnp.exp(jnp.sum(lam_q2[l].astype(F32) * lam_k2[l].astype(F32))) + LAM_INIT).reshape(1)
    o = _diff_attention(qkv, lam, bias_diag, bias_sub, subln_g[l].reshape(HEAD_W, 1), batch, seq, tq)

    hc = _conv_module(hglu, conv_w[l], conv_b[l].reshape(1, -1), conv_ln_g[l].reshape(1, -1),
                      conv_ln_b[l].reshape(1, -1), seq, 256)

    h1 = _mix(o, hc, gates, h, w_att_out[l].astype(BF16), w_conv_out[l].astype(BF16),
              w_out[l].astype(BF16), 256)

    out = _peer_block(h1, norm_ffn_g[l], peer_w_q[l], peer_sub_keys[l], peer_u[l], peer_v[l],
                      final_norm_g, 512)
    return out.reshape(batch, seq, d)
```
